```python
import jax, jax.numpy as jnp
from jax import lax
import numpy as np

D_MODEL = 2048
BATCH = 1
SEQ = 8192
DEPTH = 4
DEC_BATCH = 8
DEC_SEQ = 16
PAST_LEN = 4096

CHUNK = 64
HEAD_DIM = 128
H_A = 6
H_B = 5
H_C = 5
N_HEADS = H_A + H_B + H_C
D_MIX = N_HEADS * HEAD_DIM
D_A = H_A * HEAD_DIM
D_B = H_B * HEAD_DIM
D_C = H_C * HEAD_DIM
D_IN = 3 * D_MIX + H_A
N_PREV_CHUNKS = 8
BAND_ROWS = N_PREV_CHUNKS * CHUNK
REL_CLIP = 128
D_FF = 4 * D_MODEL
QBLOCK = 128
EPS = 1e-6
NEG_INF = -1e30

kernel_name = "hybrid_streaming_encoder_step"


def rms_norm(x, g):
    xf = x.astype(jnp.float32)
    y = xf * lax.rsqrt(jnp.mean(xf * xf, axis=-1, keepdims=True) + EPS)
    return (y * g.astype(jnp.float32)).astype(x.dtype)


def split_projection(xn, w_in):
    b, t, _ = xn.shape
    z = xn @ w_in
    sizes = [D_A] * 3 + [D_B] * 3 + [D_C] * 3
    idx = [int(i) for i in np.cumsum(sizes)]
    parts = jnp.split(z, idx, axis=-1)
    heads = [p.reshape(b, t, -1, HEAD_DIM) for p in parts[:9]]
    return (*heads, parts[9])


def sweep_query_blocks(block_fn, q_pos, *q_side):
    tq = q_pos.shape[0]
    qb = min(QBLOCK, tq)
    nb = tq // qb

    def split(a):
        return jnp.moveaxis(a.reshape(a.shape[0], nb, qb, *a.shape[2:]), 1, 0)

    out = lax.map(lambda blk: block_fn(*blk), (q_pos.reshape(nb, qb),) + tuple(split(a) for a in q_side))
    out = jnp.moveaxis(out, 0, 1)
    return out.reshape(out.shape[0], tq, *out.shape[3:])


def forgetting_attention(q, k, v, f_q, f_k, q_pos, k_pos):
    scale = HEAD_DIM ** -0.5
    f_k_t = jnp.swapaxes(f_k, 1, 2)

    def block(qp, qb, fqb):
        s = jnp.einsum('bqhd,bkhd->bhqk', qb, k).astype(jnp.float32) * scale
        s = s + jnp.swapaxes(fqb, 1, 2)[..., :, None] - f_k_t[..., None, :]
        s = jnp.where(k_pos[None, :] <= qp[:, None], s, NEG_INF)
        p = jax.nn.softmax(s, axis=-1)
        return jnp.einsum('bhqk,bkhd->bqhd', p.astype(v.dtype), v)

    return sweep_query_blocks(block, q_pos, q, f_q)


def stick_breaking_attention(q, k, v, q_pos, k_pos):
    scale = HEAD_DIM ** -0.5

    def block(qp, qb):
        z = jnp.einsum('bqhd,bkhd->bhqk', qb, k).astype(jnp.float32) * scale
        visible = k_pos[None, :] < qp[:, None]
        log_keep = jnp.where(visible, jax.nn.log_sigmoid(-z), 0.0)
        later = lax.cumsum(log_keep, axis=3, reverse=True) - log_keep
        w = jnp.where(visible, jnp.exp(jax.nn.log_sigmoid(z) + later), 0.0)
        return jnp.einsum('bhqk,bkhd->bqhd', w.astype(v.dtype), v)

    return sweep_query_blocks(block, q_pos, q)


def band_attention(q, k, v, q_pos, k_pos, rel_bias):
    scale = HEAD_DIM ** -0.5
    s = jnp.einsum('bcqhd,bckhd->bchqk', q, k).astype(jnp.float32) * scale
    dist = jnp.clip(q_pos[:, :, None] - k_pos[:, None, :], -REL_CLIP, REL_CLIP) + REL_CLIP
    bias = jnp.moveaxis(rel_bias[dist], -1, 1).astype(jnp.float32)
    s = s + bias[None]
    qc = q_pos[:, :, None] // CHUNK
    kc = k_pos[:, None, :] // CHUNK
    mask = (k_pos[:, None, :] >= 0) & (kc <= qc) & (kc >= qc - N_PREV_CHUNKS)
    s = jnp.where(mask[None, :, None], s, NEG_INF)
    p = jax.nn.softmax(s, axis=-1)
    return jnp.einsum('bchqk,bckhd->bcqhd', p.astype(v.dtype), v)


def band_gather(a):
    b, t = a.shape[:2]
    nc = t // CHUNK
    ap = jnp.pad(a, ((0, 0), (BAND_ROWS, 0), (0, 0), (0, 0)))
    ap = ap.reshape(b, nc + N_PREV_CHUNKS, CHUNK, *a.shape[2:])
    return jnp.concatenate([ap[:, i:i + nc] for i in range(N_PREV_CHUNKS + 1)], axis=2)


def project_and_normalise(xn, w_in, b_forget, g_q_fox, g_k_fox, g_q_band, g_k_band):
    q_a, k_a, v_a, q_b, k_b, v_b, q_c, k_c, v_c, f_logit = split_projection(xn, w_in)
    q_a = rms_norm(q_a, g_q_fox)
    k_a = rms_norm(k_a, g_k_fox)
    q_c = rms_norm(q_c, g_q_band)
    k_c = rms_norm(k_c, g_k_band)
    logf = jax.nn.log_sigmoid(f_logit.astype(jnp.float32) + b_forget.astype(jnp.float32))
    return q_a, k_a, v_a, q_b, k_b, v_b, q_c, k_c, v_c, logf


def prompt_mixers(xn, w_in, b_forget, g_q_fox, g_k_fox, g_q_band, g_k_band, rel_bias):
    b, t, _ = xn.shape
    q_a, k_a, v_a, q_b, k_b, v_b, q_c, k_c, v_c, logf = project_and_normalise(
        xn, w_in, b_forget, g_q_fox, g_k_fox, g_q_band, g_k_band)
    pos = jnp.arange(t, dtype=jnp.int32)
    f_cum = lax.cumsum(logf, axis=1)
    o_a = forgetting_attention(q_a, k_a, v_a, f_cum, f_cum, pos, pos)
    o_b = stick_breaking_attention(q_b, k_b, v_b, pos, pos)
    nc = t // CHUNK
    band_len = (N_PREV_CHUNKS + 1) * CHUNK
    k_pos_band = (jnp.arange(nc, dtype=jnp.int32)[:, None] - N_PREV_CHUNKS) * CHUNK + jnp.arange(band_len, dtype=jnp.int32)[None, :]
    o_c = band_attention(q_c.reshape(b, nc, CHUNK, H_C, HEAD_DIM), band_gather(k_c), band_gather(v_c),
                         pos.reshape(nc, CHUNK), k_pos_band, rel_bias).reshape(b, t, H_C, HEAD_DIM)
    keep = min(BAND_ROWS, t)
    new_a_kv = jnp.stack([k_a, v_a], axis=2)
    new_b_kv = jnp.stack([k_b, v_b], axis=2)
    new_c_kv = jnp.stack([k_c[:, t - keep:], v_c[:, t - keep:]], axis=2)
    o = jnp.concatenate([o_a, o_b, o_c], axis=2)
    return o, new_a_kv, logf, new_b_kv, new_c_kv


def sample_mixers(xn, c_a_kv, c_a_logf, c_b_kv, c_c_kv,
                  w_in, b_forget, g_q_fox, g_k_fox, g_q_band, g_k_band, rel_bias):
    b, t, _ = xn.shape
    past = c_a_kv.shape[1]
    keep = c_c_kv.shape[1]
    q_a, k_a, v_a, q_b, k_b, v_b, q_c, k_c, v_c, logf = project_and_normalise(
        xn, w_in, b_forget, g_q_fox, g_k_fox, g_q_band, g_k_band)
    q_pos = past + jnp.arange(t, dtype=jnp.int32)
    k_pos = jnp.arange(past + t, dtype=jnp.int32)
    f_cum = lax.cumsum(jnp.concatenate([c_a_logf.astype(jnp.float32), logf], axis=1), axis=1)
    ka_all = jnp.concatenate([c_a_kv[:, :, 0], k_a], axis=1)
    va_all = jnp.concatenate([c_a_kv[:, :, 1], v_a], axis=1)
    o_a = forgetting_attention(q_a, ka_all, va_all, f_cum[:, past:], f_cum, q_pos, k_pos)
    kb_all = jnp.concatenate([c_b_kv[:, :, 0], k_b], axis=1)
    vb_all = jnp.concatenate([c_b_kv[:, :, 1], v_b], axis=1)
    o_b = stick_breaking_attention(q_b, kb_all, vb_all, q_pos, k_pos)
    new_c = jnp.stack([k_c, v_c], axis=2)
    c_all = jnp.concatenate([c_c_kv, new_c], axis=1)
    kc_pos = past - keep + jnp.arange(keep + t, dtype=jnp.int32)
    o_c = band_attention(q_c[:, None], c_all[:, None, :, 0], c_all[:, None, :, 1],
                         q_pos[None], kc_pos[None], rel_bias)[:, 0]
    new_a_kv = jnp.stack([k_a, v_a], axis=2)
    new_b_kv = jnp.stack([k_b, v_b], axis=2)
    new_c_kv = c_all[:, t:]
    o = jnp.concatenate([o_a, o_b, o_c], axis=2)
    return o, new_a_kv, logf, new_b_kv, new_c_kv


def merge_heads(o, g_head_out, w_out):
    b, t = o.shape[:2]
    o = rms_norm(o, g_head_out.reshape(N_HEADS, HEAD_DIM))
    return o.reshape(b, t, D_MIX) @ w_out


def squared_relu_mlp(h, w_up, w_down):
    return jnp.square(jax.nn.relu(h @ w_up)) @ w_down


def setup_inputs(seed: int = 0) -> dict:
    key = jax.random.key(seed)
    ks = jax.random.split(key, 20)
    f32 = jnp.float32
    c_keep = min(BAND_ROWS, PAST_LEN)
    return {
        "x_prompt": jax.random.normal(ks[0], (BATCH, SEQ, D_MODEL), f32),
        "x_sample": jax.random.normal(ks[1], (DEC_BATCH, DEC_SEQ, D_MODEL), f32),
        "cache_a_kv": jax.random.normal(ks[2], (DEPTH, DEC_BATCH, PAST_LEN, 2, H_A, HEAD_DIM), f32),
        "cache_a_logf": jax.nn.log_sigmoid(2.0 + jax.random.normal(ks[3], (DEPTH, DEC_BATCH, PAST_LEN, H_A), f32)),
        "cache_b_kv": jax.random.normal(ks[4], (DEPTH, DEC_BATCH, PAST_LEN, 2, H_B, HEAD_DIM), f32),
        "cache_c_kv": jax.random.normal(ks[5], (DEPTH, DEC_BATCH, c_keep, 2, H_C, HEAD_DIM), f32),
        "norm_mix": 1.0 + 0.02 * jax.random.normal(ks[6], (DEPTH, D_MODEL), f32),
        "norm_mlp": 1.0 + 0.02 * jax.random.normal(ks[7], (DEPTH, D_MODEL), f32),
        "w_in": jax.random.normal(ks[8], (DEPTH, D_MODEL, D_IN), f32) * D_MODEL ** -0.5,
        "b_forget": 2.0 + 0.5 * jax.random.normal(ks[9], (DEPTH, H_A), f32),
        "g_q_fox": 1.0 + 0.02 * jax.random.normal(ks[10], (DEPTH, HEAD_DIM), f32),
        "g_k_fox": 1.0 + 0.02 * jax.random.normal(ks[11], (DEPTH, HEAD_DIM), f32),
        "g_q_band": 1.0 + 0.02 * jax.random.normal(ks[12], (DEPTH, HEAD_DIM), f32),
        "g_k_band": 1.0 + 0.02 * jax.random.normal(ks[13], (DEPTH, HEAD_DIM), f32),
        "rel_bias": 0.1 * jax.random.normal(ks[14], (DEPTH, 2 * REL_CLIP + 1, H_C), f32),
        "g_head_out": 1.0 + 0.02 * jax.random.normal(ks[15], (DEPTH, D_MIX), f32),
        "w_out": jax.random.normal(ks[16], (DEPTH, D_MIX, D_MODEL), f32) * D_MIX ** -0.5,
        "w_up": jax.random.normal(ks[17], (DEPTH, D_MODEL, D_FF), f32) * D_MODEL ** -0.5,
        "w_down": jax.random.normal(ks[18], (DEPTH, D_FF, D_MODEL), f32) * D_FF ** -0.5,
    }


def reference(x_prompt, x_sample, cache_a_kv, cache_a_logf, cache_b_kv, cache_c_kv,
              norm_mix, norm_mlp, w_in, b_forget, g_q_fox, g_k_fox, g_q_band, g_k_band,
              rel_bias, g_head_out, w_out, w_up, w_down):
    y_p = x_prompt
    y_s = x_sample
    a_kv_p, a_lf_p, b_kv_p, c_kv_p = [], [], [], []
    a_kv_s, a_lf_s, b_kv_s, c_kv_s = [], [], [], []
    for l in range(DEPTH):
        lw = (w_in[l], b_forget[l], g_q_fox[l], g_k_fox[l], g_q_band[l], g_k_band[l], rel_bias[l])
        o, akv, alf, bkv, ckv = prompt_mixers(rms_norm(y_p, norm_mix[l]), *lw)
        y_p = y_p + merge_heads(o, g_head_out[l], w_out[l])
        y_p = y_p + squared_relu_mlp(rms_norm(y_p, norm_mlp[l]), w_up[l], w_down[l])
        a_kv_p.append(akv); a_lf_p.append(alf); b_kv_p.append(bkv); c_kv_p.append(ckv)
        o, akv, alf, bkv, ckv = sample_mixers(rms_norm(y_s, norm_mix[l]), cache_a_kv[l], cache_a_logf[l],
                                              cache_b_kv[l], cache_c_kv[l], *lw)
        y_s = y_s + merge_heads(o, g_head_out[l], w_out[l])
        y_s = y_s + squared_relu_mlp(rms_norm(y_s, norm_mlp[l]), w_up[l], w_down[l])
        a_kv_s.append(akv); a_lf_s.append(alf); b_kv_s.append(bkv); c_kv_s.append(ckv)
    new_a_kv_p = jnp.stack(a_kv_p, axis=0)
    new_a_logf_p = jnp.stack(a_lf_p, axis=0)
    new_b_kv_p = jnp.stack(b_kv_p, axis=0)
    new_c_kv_p = jnp.stack(c_kv_p, axis=0)
    new_a_kv_s = jnp.stack(a_kv_s, axis=0)
    new_a_logf_s = jnp.stack(a_lf_s, axis=0)
    new_b_kv_s = jnp.stack(b_kv_s, axis=0)
    new_c_kv_s = jnp.stack(c_kv_s, axis=0)
    return (y_p, y_s, new_a_kv_p, new_a_logf_p, new_b_kv_p, new_c_kv_p,
            new_a_kv_s, new_a_logf_s, new_b_kv_s, new_c_kv_s)
```

```python
import functools

import jax
import jax.numpy as jnp
from jax import lax
from jax.experimental import pallas as pl
from jax.experimental.pallas import tpu as pltpu

HEAD_DIM = 128
H_A, H_B, H_C = 6, 5, 5
D_A, D_B, D_C = H_A * HEAD_DIM, H_B * HEAD_DIM, H_C * HEAD_DIM
D_MIX = D_A + D_B + D_C
CHUNK = 64
N_PREV_CHUNKS = 8
BAND_ROWS = N_PREV_CHUNKS * CHUNK
REL_CLIP = 128
EPS = 1e-6
NEG_INF = -1e30
SCALE = HEAD_DIM ** -0.5

LANES = 128
F_PAD = LANES
REL_PAD = 264
CUM_BLOCK = 256
VMEM_LIMIT = 58 * 1024 * 1024

F32 = jnp.float32
BF16 = jnp.bfloat16
NT_DIMS = (((1,), (1,)), ((), ()))


def _params(*sem):
    return pltpu.CompilerParams(dimension_semantics=sem, vmem_limit_bytes=VMEM_LIMIT)


def _split3(x):
    hi = x.astype(BF16)
    r1 = x - hi.astype(F32)
    mid = r1.astype(BF16)
    lo = (r1 - mid.astype(F32)).astype(BF16)
    return hi, mid, lo


def _softplus(z):
    return jnp.maximum(z, 0.0) + jnp.log1p(jnp.exp(-jnp.abs(z)))


def _head_rms(x, g):
    ms = jnp.mean(x * x, axis=-1, keepdims=True)
    return x * lax.rsqrt(ms + EPS) * g


def _rmsnorm_kernel(x_ref, g_ref, o_ref):
    x = x_ref[...]
    ms = jnp.mean(x * x, axis=-1, keepdims=True)
    o_ref[...] = (x * lax.rsqrt(ms + EPS) * g_ref[...]).astype(o_ref.dtype)


def rmsnorm_bf16(x, g, tm):
    m, d = x.shape
    return pl.pallas_call(
        _rmsnorm_kernel,
        grid=(m // tm,),
        in_specs=[pl.BlockSpec((tm, d), lambda i: (i, 0)),
                  pl.BlockSpec((1, d), lambda i: (0, 0))],
        out_specs=pl.BlockSpec((tm, d), lambda i: (i, 0)),
        out_shape=jax.ShapeDtypeStruct((m, d), BF16),
        compiler_params=_params("parallel"),
        name="rmsnorm",
    )(x, g)


def _inproj_kernel(x_ref, w_ref, bf_ref, gqa_ref, gka_ref, gqc_ref, gkc_ref,
                   qa_ref, ka_ref, va_ref, qb_ref, kb_ref, vb_ref, qc_ref, kc_ref, vc_ref,
                   kva_ref, kvb_ref, kvc_ref, logf_ref):
    x = x_ref[...]

    def proj(col, width):
        return jnp.dot(x, w_ref[:, col:col + width], preferred_element_type=F32)

    def normed(z, g_ref, n_heads):
        g = g_ref[...]
        return jnp.concatenate(
            [_head_rms(z[:, h * HEAD_DIM:(h + 1) * HEAD_DIM], g) for h in range(n_heads)], axis=-1)

    col = 0
    qa_ref[...] = normed(proj(col, D_A), gqa_ref, H_A).astype(BF16); col += D_A
    ka = normed(proj(col, D_A), gka_ref, H_A); col += D_A
    ka_ref[...] = ka.astype(BF16)
    kva_ref[:, :D_A] = ka
    va = proj(col, D_A); col += D_A
    va_ref[...] = va.astype(BF16)
    kva_ref[:, D_A:] = va
    qb_ref[...] = proj(col, D_B).astype(BF16); col += D_B
    kb = proj(col, D_B); col += D_B
    kb_ref[...] = kb.astype(BF16)
    kvb_ref[:, :D_B] = kb
    vb = proj(col, D_B); col += D_B
    vb_ref[...] = vb.astype(BF16)
    kvb_ref[:, D_B:] = vb
    qc_ref[...] = normed(proj(col, D_C), gqc_ref, H_C).astype(BF16); col += D_C
    kc = normed(proj(col, D_C), gkc_ref, H_C); col += D_C
    kc_ref[...] = kc.astype(BF16)
    kvc_ref[:, :D_C] = kc
    vc = proj(col, D_C); col += D_C
    vc_ref[...] = vc.astype(BF16)
    kvc_ref[:, D_C:] = vc
    f = proj(col, F_PAD) + bf_ref[...]
    logf_ref[...] = jnp.minimum(f, 0.0) - jnp.log1p(jnp.exp(-jnp.abs(f)))


def input_projection(xn, w_in, b_forget, g_q_fox, g_k_fox, g_q_band, g_k_band, tm, keep):
    m, d = xn.shape
    n_w = w_in.shape[1]
    nm = m // tm
    keep_blocks = keep // tm
    row = lambda i: (i, 0)
    const = lambda i: (0, 0)
    keep_row = lambda i: (jnp.maximum(i - (nm - keep_blocks), 0), 0)
    vec = pl.BlockSpec((1, HEAD_DIM), const)
    bf = lambda w: jax.ShapeDtypeStruct((m, w), BF16)
    out_shape = ([bf(D_A)] * 3 + [bf(D_B)] * 3 + [bf(D_C)] * 3 +
                 [jax.ShapeDtypeStruct((m, 2 * D_A), F32),
                  jax.ShapeDtypeStruct((m, 2 * D_B), F32),
                  jax.ShapeDtypeStruct((keep, 2 * D_C), F32),
                  jax.ShapeDtypeStruct((m, F_PAD), F32)])
    out_specs = ([pl.BlockSpec((tm, D_A), row)] * 3 + [pl.BlockSpec((tm, D_B), row)] * 3 +
                 [pl.BlockSpec((tm, D_C), row)] * 3 +
                 [pl.BlockSpec((tm, 2 * D_A), row), pl.BlockSpec((tm, 2 * D_B), row),
                  pl.BlockSpec((tm, 2 * D_C), keep_row), pl.BlockSpec((tm, F_PAD), row)])
    return pl.pallas_call(
        _inproj_kernel,
        grid=(nm,),
        in_specs=[pl.BlockSpec((tm, d), row),
                  pl.BlockSpec((d, n_w), const, pipeline_mode=pl.Buffered(1)),
                  vec, vec, vec, vec, vec],
        out_specs=out_specs,
        out_shape=out_shape,
        compiler_params=_params("arbitrary"),
        name="inproj",
    )(xn, w_in, b_forget, g_q_fox, g_k_fox, g_q_band, g_k_band)


def _prefix_kernel(x_ref, ft_ref, carry_ref):
    @pl.when(pl.program_id(1) == 0)
    def _():
        carry_ref[...] = jnp.zeros_like(carry_ref)

    tb = x_ref.shape[1]
    r = lax.broadcasted_iota(jnp.int32, (tb, tb), 0)
    c = lax.broadcasted_iota(jnp.int32, (tb, tb), 1)
    lower = jnp.where(c <= r, 1.0, 0.0).astype(BF16)
    cum = carry_ref[...]
    for part in _split3(x_ref[0]):
        cum = cum + jnp.dot(lower, part, preferred_element_type=F32)
    carry_ref[...] = cum[tb - 1:tb, :]
    er = lax.broadcasted_iota(jnp.int32, (8, LANES), 0)
    ec = lax.broadcasted_iota(jnp.int32, (8, LANES), 1)
    eye = jnp.where(er == ec, 1.0, 0.0).astype(BF16)
    ft = jnp.zeros((8, tb), F32)
    for part in _split3(cum):
        ft = ft + lax.dot_general(eye, part, NT_DIMS, preferred_element_type=F32)
    ft_ref[0] = ft


def forget_prefix(logf):
    b, t, _ = logf.shape
    return pl.pallas_call(
        _prefix_kernel,
        grid=(b, t // CUM_BLOCK),
        in_specs=[pl.BlockSpec((1, CUM_BLOCK, F_PAD), lambda i, j: (i, j, 0))],
        out_specs=pl.BlockSpec((1, 8, CUM_BLOCK), lambda i, j: (i, 0, j)),
        out_shape=jax.ShapeDtypeStruct((b, 8, t), F32),
        scratch_shapes=[pltpu.VMEM((1, F_PAD), F32)],
        compiler_params=_params("parallel", "arbitrary"),
        name="forget_prefix",
    )(logf)


def _fox_prompt_kernel(q_ref, k_ref, v_ref, ft_ref, g_ref, o_ref, m_ref, l_ref, acc_ref, *, tk):
    tq = q_ref.shape[0]
    qi = pl.program_id(1)
    q = q_ref[...]
    m_ref[...] = jnp.full_like(m_ref, NEG_INF)
    l_ref[...] = jnp.zeros_like(l_ref)
    acc_ref[...] = jnp.zeros_like(acc_ref)

    def step(k0, diag_offset):
        k = k_ref[pl.ds(k0, tk), :]
        v = v_ref[pl.ds(k0, tk), :]
        s = lax.dot_general(q, k, NT_DIMS, preferred_element_type=F32) * SCALE
        s = s - ft_ref[0, :, pl.ds(k0, tk)]
        if diag_offset is not None:
            r = lax.broadcasted_iota(jnp.int32, (tq, tk), 0)
            c = lax.broadcasted_iota(jnp.int32, (tq, tk), 1)
            s = jnp.where(c + diag_offset <= r, s, NEG_INF)
        m_prev = m_ref[...]
        m_new = jnp.maximum(m_prev, jnp.max(s, axis=-1, keepdims=True))
        alpha = jnp.exp(m_prev - m_new)
        p = jnp.exp(s - m_new)
        l_ref[...] = alpha * l_ref[...] + jnp.sum(p, axis=-1, keepdims=True)
        acc_ref[...] = alpha * acc_ref[...] + jnp.dot(p.astype(BF16), v, preferred_element_type=F32)
        m_ref[...] = m_new

    def body(j, carry):
        step(pl.multiple_of(j * tk, tk), None)
        return carry

    lax.fori_loop(0, qi * (tq // tk), body, 0)
    for d in range(tq // tk):
        step(pl.multiple_of(qi * tq + d * tk, tk), d * tk)
    o = acc_ref[...] / l_ref[...]
    o_ref[...] = _head_rms(o, g_ref[...]).astype(o_ref.dtype)


def fox_prompt(q, k, v, ft, g_head, tq, tk):
    t = q.shape[0]
    return pl.pallas_call(
        functools.partial(_fox_prompt_kernel, tk=tk),
        grid=(H_A, t // tq),
        in_specs=[pl.BlockSpec((tq, HEAD_DIM), lambda h, i: (i, h)),
                  pl.BlockSpec((t, HEAD_DIM), lambda h, i: (0, h)),
                  pl.BlockSpec((t, HEAD_DIM), lambda h, i: (0, h)),
                  pl.BlockSpec((1, 1, t), lambda h, i: (h, 0, 0)),
                  pl.BlockSpec((1, HEAD_DIM), lambda h, i: (0, h))],
        out_specs=pl.BlockSpec((tq, HEAD_DIM), lambda h, i: (i, h)),
        out_shape=jax.ShapeDtypeStruct((t, D_A), BF16),
        scratch_shapes=[pltpu.VMEM((tq, 1), F32), pltpu.VMEM((tq, 1), F32),
                        pltpu.VMEM((tq, HEAD_DIM), F32)],
        compiler_params=_params("parallel", "arbitrary"),
        name="fox_prompt",
    )(q, k, v, ft, g_head)


def _strict_upper(n):
    r = lax.broadcasted_iota(jnp.int32, (n, n), 0)
    c = lax.broadcasted_iota(jnp.int32, (n, n), 1)
    return jnp.where(r > c, 1.0, 0.0).astype(BF16)


def _stick_block(q, k, v, suffix_mat, carry, visible):
    z = lax.dot_general(q, k, NT_DIMS, preferred_element_type=F32) * SCALE
    log_keep = -_softplus(z)
    if visible is not None:
        log_keep = jnp.where(visible, log_keep, 0.0)
    hi = log_keep.astype(BF16)
    lo = (log_keep - hi.astype(F32)).astype(BF16)
    later = (jnp.dot(hi, suffix_mat, preferred_element_type=F32) +
             jnp.dot(lo, suffix_mat, preferred_element_type=F32)) + carry
    w = jnp.exp(z + log_keep + later)
    if visible is not None:
        w = jnp.where(visible, w, 0.0)
    pv = jnp.dot(w.astype(BF16), v, preferred_element_type=F32)
    return pv, carry + jnp.sum(log_keep, axis=-1, keepdims=True)


def _stick_prompt_kernel(q_ref, k_ref, v_ref, g_ref, o_ref, c_ref, acc_ref):
    tq = q_ref.shape[0]
    tb = CUM_BLOCK
    qi = pl.program_id(1)
    q = q_ref[...]
    suffix_mat = _strict_upper(tb)
    c_ref[...] = jnp.zeros_like(c_ref)
    acc_ref[...] = jnp.zeros_like(acc_ref)

    def step(k0, diag_offset):
        visible = None
        if diag_offset is not None:
            r = lax.broadcasted_iota(jnp.int32, (tq, tb), 0)
            c = lax.broadcasted_iota(jnp.int32, (tq, tb), 1)
            visible = c + diag_offset < r
        pv, carry = _stick_block(q, k_ref[pl.ds(k0, tb), :], v_ref[pl.ds(k0, tb), :],
                                 suffix_mat, c_ref[...], visible)
        acc_ref[...] += pv
        c_ref[...] = carry

    for d in reversed(range(tq // tb)):
        step(pl.multiple_of(qi * tq + d * tb, tb), d * tb)
    n_full = qi * (tq // tb)

    def body(j, carry):
        step(pl.multiple_of((n_full - 1 - j) * tb, tb), None)
        return carry

    lax.fori_loop(0, n_full, body, 0)
    o_ref[...] = _head_rms(acc_ref[...], g_ref[...]).astype(o_ref.dtype)


def stick_prompt(q, k, v, g_head, tq):
    t = q.shape[0]
    return pl.pallas_call(
        _stick_prompt_kernel,
        grid=(H_B, t // tq),
        in_specs=[pl.BlockSpec((tq, HEAD_DIM), lambda h, i: (i, h)),
                  pl.BlockSpec((t, HEAD_DIM), lambda h, i: (0, h)),
                  pl.BlockSpec((t, HEAD_DIM), lambda h, i: (0, h)),
                  pl.BlockSpec((1, HEAD_DIM), lambda h, i: (0, H_A + h))],
        out_specs=pl.BlockSpec((tq, HEAD_DIM), lambda h, i: (i, h)),
        out_shape=jax.ShapeDtypeStruct((t, D_B), BF16),
        scratch_shapes=[pltpu.VMEM((tq, 1), F32), pltpu.VMEM((tq, HEAD_DIM), F32)],
        compiler_params=_params("parallel", "arbitrary"),
        name="stick_prompt",
    )(q, k, v, g_head)


def _band_table_kernel(rb_ref, o_ref):
    rows, cols = o_ref.shape[1], o_ref.shape[2]
    m = lax.broadcasted_iota(jnp.int32, (REL_PAD, cols), 1)
    e = lax.broadcasted_iota(jnp.int32, (REL_PAD, cols), 0)
    d = ((BAND_ROWS - m + (CHUNK - 1)) & (cols - 1)) - (CHUNK - 1)
    idx = jnp.clip(d, -REL_CLIP, REL_CLIP) + REL_CLIP
    row0 = jnp.sum(jnp.where(e == idx, rb_ref[0], 0.0), axis=0, keepdims=True)
    x = jnp.broadcast_to(row0, (rows, cols))
    i = lax.broadcasted_iota(jnp.int32, (rows, cols), 0)
    j = lax.broadcasted_iota(jnp.int32, (rows, cols), 1)
    shift = 1
    while shift < rows:
        x = jnp.where((i & shift) != 0, pltpu.roll(x, shift, axis=1), x)
        shift *= 2
    qc = i // CHUNK
    kc = j // CHUNK - N_PREV_CHUNKS
    allowed = (kc <= qc) & (kc >= qc - N_PREV_CHUNKS)
    o_ref[0] = jnp.where(allowed, x, NEG_INF)


def band_table(rel_bias):
    rb = jnp.pad(rel_bias.T, ((0, 0), (0, REL_PAD - rel_bias.shape[0])))[:, :, None]
    return pl.pallas_call(
        _band_table_kernel,
        grid=(H_C,),
        in_specs=[pl.BlockSpec((1, REL_PAD, 1), lambda h: (h, 0, 0))],
        out_specs=pl.BlockSpec((1, BAND_ROWS, 2 * BAND_ROWS), lambda h: (h, 0, 0)),
        out_shape=jax.ShapeDtypeStruct((H_C, BAND_ROWS, 2 * BAND_ROWS), F32),
        compiler_params=_params("parallel"),
        name="band_table",
    )(rb)


def _band_prompt_kernel(q_ref, kp_ref, kc_ref, vp_ref, vc_ref, tab_ref, g_ref, o_ref):
    tq = q_ref.shape[0]
    qi = pl.program_id(1)
    q = q_ref[...]
    s_prev = lax.dot_general(q, kp_ref[...], NT_DIMS, preferred_element_type=F32) * SCALE
    s_cur = lax.dot_general(q, kc_ref[...], NT_DIMS, preferred_element_type=F32) * SCALE
    s_prev = jnp.where(qi > 0, s_prev + tab_ref[0, :, :tq], NEG_INF)
    s_cur = s_cur + tab_ref[0, :, tq:]
    m = jnp.maximum(jnp.max(s_prev, axis=-1, keepdims=True), jnp.max(s_cur, axis=-1, keepdims=True))
    p_prev = jnp.exp(s_prev - m)
    p_cur = jnp.exp(s_cur - m)
    l = jnp.sum(p_prev, axis=-1, keepdims=True) + jnp.sum(p_cur, axis=-1, keepdims=True)
    o = (jnp.dot(p_prev.astype(BF16), vp_ref[...], preferred_element_type=F32) +
         jnp.dot(p_cur.astype(BF16), vc_ref[...], preferred_element_type=F32)) / l
    o_ref[...] = _head_rms(o, g_ref[...]).astype(o_ref.dtype)


def band_prompt(q, k, v, table, g_head):
    t = q.shape[0]
    tq = BAND_ROWS
    cur = lambda h, i: (i, h)
    prev = lambda h, i: (jnp.maximum(i - 1, 0), h)
    blk = lambda f: pl.BlockSpec((tq, HEAD_DIM), f)
    return pl.pallas_call(
        _band_prompt_kernel,
        grid=(H_C, t // tq),
        in_specs=[blk(cur), blk(prev), blk(cur), blk(prev), blk(cur),
                  pl.BlockSpec((1, tq, 2 * tq), lambda h, i: (h, 0, 0)),
                  pl.BlockSpec((1, HEAD_DIM), lambda h, i: (0, H_A + H_B + h))],
        out_specs=blk(cur),
        out_shape=jax.ShapeDtypeStruct((t, D_C), BF16),
        compiler_params=_params("parallel", "arbitrary"),
        name="band_prompt",
    )(q, k, k, v, v, table, g_head)


def _fox_sample_kernel(q_ref, cache_ref, kt_ref, vt_ref, ftc_ref, ftt_ref, g_ref, o_ref,
                       m_ref, l_ref, acc_ref, *, n_new):
    j = pl.program_id(1)
    tq = q_ref.shape[0]

    @pl.when(j == 0)
    def _():
        m_ref[...] = jnp.full_like(m_ref, NEG_INF)
        l_ref[...] = jnp.zeros_like(l_ref)
        acc_ref[...] = jnp.zeros_like(acc_ref)

    def update(h, s, v):
        m_prev = m_ref[h]
        m_new = jnp.maximum(m_prev, jnp.max(s, axis=-1, keepdims=True))
        alpha = jnp.exp(m_prev - m_new)
        p = jnp.exp(s - m_new)
        l_ref[h] = alpha * l_ref[h] + jnp.sum(p, axis=-1, keepdims=True)
        acc_ref[h] = alpha * acc_ref[h] + jnp.dot(p.astype(BF16), v, preferred_element_type=F32)
        m_ref[h] = m_new

    for h in range(H_A):
        lo, hi = h * HEAD_DIM, (h + 1) * HEAD_DIM
        q = q_ref[:, lo:hi]
        k = cache_ref[0, :, lo:hi].astype(BF16)
        v = cache_ref[0, :, D_A + lo:D_A + hi].astype(BF16)
        s = lax.dot_general(q, k, NT_DIMS, preferred_element_type=F32) * SCALE
        update(h, s - ftc_ref[0, h:h + 1, :], v)

    @pl.when(j == pl.num_programs(1) - 1)
    def _():
        tt = kt_ref.shape[1]
        r = lax.broadcasted_iota(jnp.int32, (tq, tt), 0)
        c = lax.broadcasted_iota(jnp.int32, (tq, tt), 1)
        visible = (c <= r) & (c < n_new)
        outs = []
        for h in range(H_A):
            lo, hi = h * HEAD_DIM, (h + 1) * HEAD_DIM
            s = lax.dot_general(q_ref[:, lo:hi], kt_ref[0, :, lo:hi], NT_DIMS,
                                preferred_element_type=F32) * SCALE
            s = jnp.where(visible, s - ftt_ref[0, h:h + 1, :tt], NEG_INF)
            update(h, s, vt_ref[0, :, lo:hi])
            outs.append(_head_rms(acc_ref[h] / l_ref[h], g_ref[:, lo:hi]))
        o_ref[...] = jnp.concatenate(outs, axis=-1).astype(o_ref.dtype)


def fox_sample(q, cache, k_tail, v_tail, ft, g_head, n_new, tk):
    b, past, _ = cache.shape
    tt = k_tail.shape[1]
    return pl.pallas_call(
        functools.partial(_fox_sample_kernel, n_new=n_new),
        grid=(b, past // tk),
        in_specs=[pl.BlockSpec((n_new, D_A), lambda i, j: (i, 0)),
                  pl.BlockSpec((1, tk, 2 * D_A), lambda i, j: (i, j, 0)),
                  pl.BlockSpec((1, tt, D_A), lambda i, j: (i, 0, 0)),
                  pl.BlockSpec((1, tt, D_A), lambda i, j: (i, 0, 0)),
                  pl.BlockSpec((1, 8, tk), lambda i, j: (i, 0, j)),
                  pl.BlockSpec((1, 8, CUM_BLOCK), lambda i, j: (i, 0, past // CUM_BLOCK)),
                  pl.BlockSpec((1, D_MIX), lambda i, j: (0, 0))],
        out_specs=pl.BlockSpec((n_new, D_A), lambda i, j: (i, 0)),
        out_shape=jax.ShapeDtypeStruct((b * n_new, D_A), BF16),
        scratch_shapes=[pltpu.VMEM((H_A, n_new, 1), F32), pltpu.VMEM((H_A, n_new, 1), F32),
                        pltpu.VMEM((H_A, n_new, HEAD_DIM), F32)],
        compiler_params=_params("parallel", "arbitrary"),
        name="fox_sample",
    )(q, cache, k_tail, v_tail, ft, ft, g_head)


def _stick_sample_kernel(q_ref, cache_ref, kt_ref, vt_ref, g_ref, o_ref, c_ref, acc_ref, *, n_new):
    j = pl.program_id(1)
    tq = q_ref.shape[0]
    tk = cache_ref.shape[1]
    tb = CUM_BLOCK
    suffix_mat = _strict_upper(tb)

    @pl.when(j == 0)
    def _():
        tt = kt_ref.shape[1]
        r = lax.broadcasted_iota(jnp.int32, (tq, tt), 0)
        c = lax.broadcasted_iota(jnp.int32, (tq, tt), 1)
        visible = (c < r) & (c < n_new)
        tail_mat = _strict_upper(tt)
        for h in range(H_B):
            lo, hi = h * HEAD_DIM, (h + 1) * HEAD_DIM
            pv, carry = _stick_block(q_ref[:, lo:hi], kt_ref[0, :, lo:hi], vt_ref[0, :, lo:hi],
                                     tail_mat, jnp.zeros((tq, 1), F32), visible)
            acc_ref[h] = pv
            c_ref[h] = carry

    for h in range(H_B):
        lo, hi = h * HEAD_DIM, (h + 1) * HEAD_DIM
        q = q_ref[:, lo:hi]
        for sb in reversed(range(tk // tb)):
            k = cache_ref[0, sb * tb:(sb + 1) * tb, lo:hi].astype(BF16)
            v = cache_ref[0, sb * tb:(sb + 1) * tb, D_B + lo:D_B + hi].astype(BF16)
            pv, carry = _stick_block(q, k, v, suffix_mat, c_ref[h], None)
            acc_ref[h] += pv
            c_ref[h] = carry

    @pl.when(j == pl.num_programs(1) - 1)
    def _():
        outs = [_head_rms(acc_ref[h], g_ref[:, D_A + h * HEAD_DIM:D_A + (h + 1) * HEAD_DIM])
                for h in range(H_B)]
        o_ref[...] = jnp.concatenate(outs, axis=-1).astype(o_ref.dtype)


def stick_sample(q, cache, k_tail, v_tail, g_head, n_new, tk):
    b, past, _ = cache.shape
    tt = k_tail.shape[1]
    nk = past // tk
    return pl.pallas_call(
        functools.partial(_stick_sample_kernel, n_new=n_new),
        grid=(b, nk),
        in_specs=[pl.BlockSpec((n_new, D_B), lambda i, j: (i, 0)),
                  pl.BlockSpec((1, tk, 2 * D_B), lambda i, j: (i, nk - 1 - j, 0)),
                  pl.BlockSpec((1, tt, D_B), lambda i, j: (i, 0, 0)),
                  pl.BlockSpec((1, tt, D_B), lambda i, j: (i, 0, 0)),
                  pl.BlockSpec((1, D_MIX), lambda i, j: (0, 0))],
        out_specs=pl.BlockSpec((n_new, D_B), lambda i, j: (i, 0)),
        out_shape=jax.ShapeDtypeStruct((b * n_new, D_B), BF16),
        scratch_shapes=[pltpu.VMEM((H_B, n_new, 1), F32), pltpu.VMEM((H_B, n_new, HEAD_DIM), F32)],
        compiler_params=_params("parallel", "arbitrary"),
        name="stick_sample",
    )(q, cache, k_tail, v_tail, g_head)


def _band_sample_kernel(q_ref, cache_ref, kt_ref, vt_ref, new_ref, tab_ref, g_ref, o_ref, kv_ref,
                        *, n_new):
    keep = cache_ref.shape[1]
    tt = kt_ref.shape[1]
    tq = q_ref.shape[0]
    c = lax.broadcasted_iota(jnp.int32, (tq, tt), 1)
    outs = []
    for h in range(H_C):
        lo, hi = h * HEAD_DIM, (h + 1) * HEAD_DIM
        q = q_ref[:, lo:hi]
        k = cache_ref[0, :, lo:hi].astype(BF16)
        v = cache_ref[0, :, D_C + lo:D_C + hi].astype(BF16)
        s_c = lax.dot_general(q, k, NT_DIMS, preferred_element_type=F32) * SCALE + tab_ref[h, :, :keep]
        s_t = lax.dot_general(q, kt_ref[0, :, lo:hi], NT_DIMS, preferred_element_type=F32) * SCALE
        s_t = jnp.where(c < n_new, s_t + tab_ref[h, :, keep:], NEG_INF)
        m = jnp.maximum(jnp.max(s_c, axis=-1, keepdims=True), jnp.max(s_t, axis=-1, keepdims=True))
        p_c = jnp.exp(s_c - m)
        p_t = jnp.exp(s_t - m)
        l = jnp.sum(p_c, axis=-1, keepdims=True) + jnp.sum(p_t, axis=-1, keepdims=True)
        o = (jnp.dot(p_c.astype(BF16), v, preferred_element_type=F32) +
             jnp.dot(p_t.astype(BF16), vt_ref[0, :, lo:hi], preferred_element_type=F32)) / l
        outs.append(_head_rms(o, g_ref[:, D_A + D_B + lo:D_A + D_B + hi]))
    o_ref[...] = jnp.concatenate(outs, axis=-1).astype(o_ref.dtype)
    kv_ref[0, :keep - n_new, :] = cache_ref[0, n_new:, :]
    kv_ref[0, keep - n_new:, :] = new_ref[...]


def band_sample(q, cache, k_tail, v_tail, kv_new, table, g_head, n_new):
    b, keep, _ = cache.shape
    tt = k_tail.shape[1]
    return pl.pallas_call(
        functools.partial(_band_sample_kernel, n_new=n_new),
        grid=(b,),
        in_specs=[pl.BlockSpec((n_new, D_C), lambda i: (i, 0)),
                  pl.BlockSpec((1, keep, 2 * D_C), lambda i: (i, 0, 0)),
                  pl.BlockSpec((1, tt, D_C), lambda i: (i, 0, 0)),
                  pl.BlockSpec((1, tt, D_C), lambda i: (i, 0, 0)),
                  pl.BlockSpec((n_new, 2 * D_C), lambda i: (i, 0)),
                  pl.BlockSpec((H_C, n_new, keep + tt), lambda i: (0, 0, 0)),
                  pl.BlockSpec((1, D_MIX), lambda i: (0, 0))],
        out_specs=[pl.BlockSpec((n_new, D_C), lambda i: (i, 0)),
                   pl.BlockSpec((1, keep, 2 * D_C), lambda i: (i, 0, 0))],
        out_shape=[jax.ShapeDtypeStruct((b * n_new, D_C), BF16),
                   jax.ShapeDtypeStruct((b, keep, 2 * D_C), F32)],
        compiler_params=_params("parallel"),
        name="band_sample",
    )(q, cache, k_tail, v_tail, kv_new, table, g_head)


def _outproj_kernel(x_ref, oa_ref, ob_ref, oc_ref, w_ref, g_ref, y_ref, hn_ref):
    y = x_ref[...]
    y = y + jnp.dot(oa_ref[...], w_ref[:D_A, :], preferred_element_type=F32)
    y = y + jnp.dot(ob_ref[...], w_ref[D_A:D_A + D_B, :], preferred_element_type=F32)
    y = y + jnp.dot(oc_ref[...], w_ref[D_A + D_B:, :], preferred_element_type=F32)
    y_ref[...] = y
    ms = jnp.mean(y * y, axis=-1, keepdims=True)
    hn_ref[...] = (y * lax.rsqrt(ms + EPS) * g_ref[...]).astype(hn_ref.dtype)


def output_projection(x, oa, ob, oc, w_out, g_mlp, tm):
    m, d = x.shape
    row = lambda i: (i, 0)
    const = lambda i: (0, 0)
    return pl.pallas_call(
        _outproj_kernel,
        grid=(m // tm,),
        in_specs=[pl.BlockSpec((tm, d), row), pl.BlockSpec((tm, D_A), row),
                  pl.BlockSpec((tm, D_B), row), pl.BlockSpec((tm, D_C), row),
                  pl.BlockSpec((D_MIX, d), const, pipeline_mode=pl.Buffered(1)),
                  pl.BlockSpec((1, d), const)],
        out_specs=[pl.BlockSpec((tm, d), row), pl.BlockSpec((tm, d), row)],
        out_shape=[jax.ShapeDtypeStruct((m, d), F32), jax.ShapeDtypeStruct((m, d), BF16)],
        compiler_params=_params("parallel"),
        name="outproj",
    )(x, oa, ob, oc, w_out, g_mlp)


def _mlp_kernel(h_ref, y_ref, wu_ref, wd_ref, g_ref, o_ref, xn_ref, acc_ref):
    f = pl.program_id(1)

    @pl.when(f == 0)
    def _():
        acc_ref[...] = y_ref[...]

    u = jnp.dot(h_ref[...], wu_ref[...], preferred_element_type=F32)
    u = jnp.square(jnp.maximum(u, 0.0)).astype(BF16)
    acc_ref[...] += jnp.dot(u, wd_ref[...], preferred_element_type=F32)

    @pl.when(f == pl.num_programs(1) - 1)
    def _():
        y = acc_ref[...]
        o_ref[...] = y
        ms = jnp.mean(y * y, axis=-1, keepdims=True)
        xn_ref[...] = (y * lax.rsqrt(ms + EPS) * g_ref[...]).astype(xn_ref.dtype)


def mlp_block(hn, y, w_up, w_down, g_next, tm, tf):
    m, d = y.shape
    d_ff = w_up.shape[1]
    return pl.pallas_call(
        _mlp_kernel,
        grid=(m // tm, d_ff // tf),
        in_specs=[pl.BlockSpec((tm, d), lambda i, f: (i, 0)),
                  pl.BlockSpec((tm, d), lambda i, f: (i, 0)),
                  pl.BlockSpec((d, tf), lambda i, f: (0, f)),
                  pl.BlockSpec((tf, d), lambda i, f: (f, 0)),
                  pl.BlockSpec((1, d), lambda i, f: (0, 0))],
        out_specs=[pl.BlockSpec((tm, d), lambda i, f: (i, 0)),
                   pl.BlockSpec((tm, d), lambda i, f: (i, 0))],
        out_shape=[jax.ShapeDtypeStruct((m, d), F32), jax.ShapeDtypeStruct((m, d), BF16)],
        scratch_shapes=[pltpu.VMEM((tm, d), F32)],
        compiler_params=_params("parallel", "arbitrary"),
        name="mlp",
    )(hn, y, w_up, w_down, g_next)


def _tiles(t_prompt):
    return dict(
        norm=min(512, t_prompt), inproj=min(256, t_prompt), attn_q=min(512, t_prompt),
        fox_k=min(512, t_prompt), outproj=min(512, t_prompt), mlp_m=min(512, t_prompt), mlp_f=1024,
        sample_k=1024)


def kernel(x_prompt, x_sample, cache_a_kv, cache_a_logf, cache_b_kv, cache_c_kv, norm_mix, norm_mlp,
           w_in, b_forget, g_q_fox, g_k_fox, g_q_band, g_k_band, rel_bias, g_head_out, w_out, w_up,
           w_down):
    depth = w_in.shape[0]
    bp, t, d = x_prompt.shape
    bs, ts, _ = x_sample.shape
    past = cache_a_kv.shape[2]
    keep_s = cache_c_kv.shape[2]
    keep_p = min(BAND_ROWS, t)
    assert bp == 1 and t % BAND_ROWS == 0, "prompt group: one stream, whole band tiles"
    assert past % CHUNK == 0 and keep_s == BAND_ROWS and ts <= CHUNK and ts % 8 == 0
    assert w_in.shape[2] == 3 * D_MIX + H_A
    tl = _tiles(t)
    ms = bs * ts
    tk_s = min(tl["sample_k"], past)

    w_in_b = jnp.pad(w_in.astype(BF16), ((0, 0), (0, 0), (0, F_PAD - H_A)))
    w_out_b = w_out.astype(BF16)
    w_up_b = w_up.astype(BF16)
    w_down_b = w_down.astype(BF16)
    b_forget_p = jnp.pad(b_forget, ((0, 0), (0, F_PAD - H_A)))[:, None, :]
    cache_a = cache_a_kv.reshape(depth, bs, past, 2 * D_A)
    cache_b = cache_b_kv.reshape(depth, bs, past, 2 * D_B)
    cache_c = cache_c_kv.reshape(depth, bs, keep_s, 2 * D_C)
    cache_lf = jnp.pad(cache_a_logf, ((0, 0), (0, 0), (0, 0), (0, F_PAD - H_A)))

    y_p = x_prompt.reshape(t, d)
    y_s = x_sample.reshape(ms, d)
    xn_p = rmsnorm_bf16(y_p, norm_mix[0][None], tl["norm"])
    xn_s = rmsnorm_bf16(y_s, norm_mix[0][None], ms)
    outs = [[] for _ in range(8)]

    def tail(a):
        return jnp.pad(a.reshape(bs, ts, a.shape[-1]), ((0, 0), (0, LANES - ts), (0, 0)))

    for l in range(depth):
        vecs = (b_forget_p[l], g_q_fox[l][None], g_k_fox[l][None], g_q_band[l][None], g_k_band[l][None])
        g_head = g_head_out[l][None]
        g_next = norm_mix[(l + 1) % depth][None]
        table = band_table(rel_bias[l])

        qa, ka, va, qb, kb, vb, qc, kc, vc, kva, kvb, kvc, logf = input_projection(
            xn_p, w_in_b[l], *vecs, tm=tl["inproj"], keep=keep_p)
        ft = forget_prefix(logf[None]).reshape(8, 1, t)
        oa = fox_prompt(qa, ka, va, ft, g_head, tl["attn_q"], tl["fox_k"])
        ob = stick_prompt(qb, kb, vb, g_head, tl["attn_q"])
        oc = band_prompt(qc, kc, vc, table, g_head)
        y_p, hn = output_projection(y_p, oa, ob, oc, w_out_b[l], norm_mlp[l][None], tl["outproj"])
        y_p, xn_p = mlp_block(hn, y_p, w_up_b[l], w_down_b[l], g_next, tl["mlp_m"], tl["mlp_f"])
        outs[0].append(kva.reshape(1, t, 2, H_A, HEAD_DIM))
        outs[1].append(logf[:, :H_A].reshape(1, t, H_A))
        outs[2].append(kvb.reshape(1, t, 2, H_B, HEAD_DIM))
        outs[3].append(kvc.reshape(1, keep_p, 2, H_C, HEAD_DIM))

        qa, ka, va, qb, kb, vb, qc, kc, vc, kva, kvb, kvc, logf = input_projection(
            xn_s, w_in_b[l], *vecs, tm=ms, keep=ms)
        lf_all = jnp.concatenate(
            [cache_lf[l], logf.reshape(bs, ts, F_PAD),
             jnp.zeros((bs, CUM_BLOCK - ts, F_PAD), F32)], axis=1)
        ft = forget_prefix(lf_all)
        oa = fox_sample(qa, cache_a[l], tail(ka), tail(va), ft, g_head, ts, tk_s)
        ob = stick_sample(qb, cache_b[l], tail(kb), tail(vb), g_head, ts, tk_s)
        oc, kvc_roll = band_sample(qc, cache_c[l], tail(kc), tail(vc), kvc, table, g_head, ts)
        y_s, hn = output_projection(y_s, oa, ob, oc, w_out_b[l], norm_mlp[l][None], ms)
        y_s, xn_s = mlp_block(hn, y_s, w_up_b[l], w_down_b[l], g_next, ms, tl["mlp_f"])
        outs[4].append(kva.reshape(bs, ts, 2, H_A, HEAD_DIM))
        outs[5].append(logf[:, :H_A].reshape(bs, ts, H_A))
        outs[6].append(kvb.reshape(bs, ts, 2, H_B, HEAD_DIM))
        outs[7].append(kvc_roll.reshape(bs, keep_s, 2, H_C, HEAD_DIM))

    stacked = [jnp.stack(o, axis=0) for o in outs]
    return (y_p.reshape(bp, t, d), y_s.reshape(bs, ts, d), *stacked)
```

```python
import functools
import math

import jax
import jax.numpy as jnp
from jax import lax
from jax.experimental import pallas as pl
from jax.experimental.pallas import tpu as pltpu

HEAD_DIM = 128
H_A, H_B, H_C = 6, 5, 5
D_A, D_B, D_C = H_A * HEAD_DIM, H_B * HEAD_DIM, H_C * HEAD_DIM
D_MIX = D_A + D_B + D_C
CHUNK = 64
N_PREV_CHUNKS = 8
BAND_ROWS = N_PREV_CHUNKS * CHUNK
REL_CLIP = 128
EPS = 1e-6
NEG_INF = -1e30
LOG2E = math.log2(math.e)
SCALE = HEAD_DIM ** -0.5
SCALE2 = SCALE * LOG2E

LANES = 128
SUBLANES = 8
REL_PAD = 264
CUM_BLOCK = 256
VMEM_LIMIT = 58 * 1024 * 1024

F32 = jnp.float32
BF16 = jnp.bfloat16
NT_DIMS = (((1,), (1,)), ((), ()))


def _params(*sem):
    return pltpu.CompilerParams(dimension_semantics=sem, vmem_limit_bytes=VMEM_LIMIT)


def _split3(x):
    hi = x.astype(BF16)
    r1 = x - hi.astype(F32)
    mid = r1.astype(BF16)
    lo = (r1 - mid.astype(F32)).astype(BF16)
    return hi, mid, lo


def _dot_exact(x, mat):
    acc = None
    for part in _split3(x):
        d = jnp.dot(part, mat, preferred_element_type=F32)
        acc = d if acc is None else acc + d
    return acc


def _tri(n, strict):
    r = lax.broadcasted_iota(jnp.int32, (n, n), 0)
    c = lax.broadcasted_iota(jnp.int32, (n, n), 1)
    return jnp.where(r < c if strict else r <= c, 1.0, 0.0).astype(BF16)


def _lane_tile(x, width):
    return jnp.concatenate([x] * (width // LANES), axis=1) if width > LANES else x


def _head_rms(x, g):
    ms = jnp.mean(x * x, axis=-1, keepdims=True)
    return x * lax.rsqrt(ms + EPS) * g


def _row_rms(y, g, dtype):
    ms = jnp.mean(y * y, axis=-1, keepdims=True)
    return (y * lax.rsqrt(ms + EPS) * g).astype(dtype)


def _rmsnorm_kernel(x_ref, g_ref, o_ref):
    o_ref[...] = _row_rms(x_ref[...], g_ref[...], o_ref.dtype)


def rmsnorm_bf16(x, g, tm):
    m, d = x.shape
    return pl.pallas_call(
        _rmsnorm_kernel,
        grid=(m // tm,),
        in_specs=[pl.BlockSpec((tm, d), lambda i: (i, 0)),
                  pl.BlockSpec((1, d), lambda i: (0, 0))],
        out_specs=pl.BlockSpec((tm, d), lambda i: (i, 0)),
        out_shape=jax.ShapeDtypeStruct((m, d), BF16),
        compiler_params=_params("parallel"),
        name="rmsnorm",
    )(x, g)


def _inproj_kernel(x_ref, w_ref, wf_ref, bf_ref, gqa_ref, gka_ref, gqc_ref, gkc_ref, *refs):
    (qa_ref, ka_ref, va_ref, qb_ref, kb_ref, vb_ref, qc_ref, kc_ref, vc_ref,
     kva_ref, kvb_ref, kvc_ref, logf_ref) = refs[-13:]
    x = x_ref[...]
    tm = x.shape[0]

    def group(col, n_heads, g_ref, bf_out, kv_out=None, kv=0, q_scale=None):
        z = jnp.dot(x, w_ref[:, col:col + n_heads * HEAD_DIM], preferred_element_type=F32)
        for h in range(n_heads):
            zh = z[:, h * HEAD_DIM:(h + 1) * HEAD_DIM]
            if g_ref is not None:
                zh = _head_rms(zh, g_ref[...])
            if kv_out is not None:
                kv_out[pl.ds(2 * h + kv, tm, stride=2 * n_heads), :] = zh
            if q_scale is not None:
                zh = zh * q_scale
            bf_out[:, h * HEAD_DIM:(h + 1) * HEAD_DIM] = zh.astype(BF16)
        return col + n_heads * HEAD_DIM

    col = 0
    col = group(col, H_A, gqa_ref, qa_ref, q_scale=SCALE2)
    col = group(col, H_A, gka_ref, ka_ref, kva_ref, 0)
    col = group(col, H_A, None, va_ref, kva_ref, 1)
    col = group(col, H_B, None, qb_ref, q_scale=-SCALE2)
    col = group(col, H_B, None, kb_ref, kvb_ref, 0)
    col = group(col, H_B, None, vb_ref, kvb_ref, 1)
    col = group(col, H_C, gqc_ref, qc_ref, q_scale=SCALE)
    col = group(col, H_C, gkc_ref, kc_ref, kvc_ref, 0)
    col = group(col, H_C, None, vc_ref, kvc_ref, 1)
    f = lax.dot_general(wf_ref[...], x, NT_DIMS, preferred_element_type=F32) + bf_ref[...]
    logf_ref[...] = jnp.minimum(f, 0.0) - jnp.log1p(jnp.exp(-jnp.abs(f)))


def input_projection(xn, w_in, w_f, b_f, g_q_fox, g_k_fox, g_q_band, g_k_band, stacked, layer, tm, keep):
    m, d = xn.shape
    depth = w_in.shape[0]
    nm = m // tm
    keep_blocks = keep // tm
    row = lambda i: (i, 0)
    const = lambda i: (0, 0)
    lrow = lambda i: (layer, i, 0)
    vec = pl.BlockSpec((1, HEAD_DIM), const)
    bf = lambda w: jax.ShapeDtypeStruct((m, w), BF16)
    out_shape = ([bf(D_A)] * 3 + [bf(D_B)] * 3 + [bf(D_C)] * 3 +
                 [jax.ShapeDtypeStruct((depth, m * 2 * H_A, HEAD_DIM), F32),
                  jax.ShapeDtypeStruct((depth, m * 2 * H_B, HEAD_DIM), F32),
                  jax.ShapeDtypeStruct((depth, keep * 2 * H_C, HEAD_DIM), F32),
                  jax.ShapeDtypeStruct((depth, SUBLANES, m), F32)])
    out_specs = ([pl.BlockSpec((tm, D_A), row)] * 3 + [pl.BlockSpec((tm, D_B), row)] * 3 +
                 [pl.BlockSpec((tm, D_C), row)] * 3 +
                 [pl.BlockSpec((None, tm * 2 * H_A, HEAD_DIM), lrow),
                  pl.BlockSpec((None, tm * 2 * H_B, HEAD_DIM), lrow),
                  pl.BlockSpec((None, tm * 2 * H_C, HEAD_DIM),
                               lambda i: (layer, jnp.maximum(i - (nm - keep_blocks), 0), 0)),
                  pl.BlockSpec((None, SUBLANES, tm), lambda i: (layer, 0, i))])
    in_specs = [pl.BlockSpec((tm, d), row),
                pl.BlockSpec((None, d, 3 * D_MIX), lambda i: (layer, 0, 0), pipeline_mode=pl.Buffered(1)),
                pl.BlockSpec((None, SUBLANES, d), lambda i: (layer, 0, 0)),
                pl.BlockSpec((None, SUBLANES, 1), lambda i: (layer, 0, 0)),
                vec, vec, vec, vec]
    args = [xn, w_in, w_f, b_f, g_q_fox, g_k_fox, g_q_band, g_k_band]
    aliases = {}
    if stacked is not None:
        in_specs += [pl.BlockSpec(memory_space=pl.ANY)] * 4
        aliases = {len(args) + n: 9 + n for n in range(4)}
        args += list(stacked)
    return pl.pallas_call(
        _inproj_kernel,
        grid=(nm,),
        in_specs=in_specs,
        out_specs=out_specs,
        out_shape=out_shape,
        input_output_aliases=aliases,
        compiler_params=_params("arbitrary"),
        name="inproj",
    )(*args)


def _prefix_kernel(x_ref, ft_ref, carry_ref):
    @pl.when(pl.program_id(0) == 0)
    def _():
        carry_ref[...] = jnp.zeros_like(carry_ref)

    tb = x_ref.shape[1]
    cum = _lane_tile(carry_ref[...], tb) + _dot_exact(x_ref[...], _tri(tb, strict=False))
    ft_ref[...] = cum * LOG2E
    carry_ref[...] = jnp.broadcast_to(cum[:, tb - 1:tb], carry_ref.shape)


def forget_prefix(logf_t, layer, tb):
    _, rows, t = logf_t.shape
    return pl.pallas_call(
        _prefix_kernel,
        grid=(t // tb,),
        in_specs=[pl.BlockSpec((None, rows, tb), lambda j: (layer, 0, j))],
        out_specs=pl.BlockSpec((rows, tb), lambda j: (0, j)),
        out_shape=jax.ShapeDtypeStruct((rows, t), F32),
        scratch_shapes=[pltpu.VMEM((rows, LANES), F32)],
        compiler_params=_params("arbitrary"),
        name="forget_prefix",
    )(logf_t)


def _fox_update(u, v, m_prev, l_prev, acc_prev):
    keys = u.shape[1]
    m_new = jnp.maximum(m_prev, jnp.max(u, axis=-1, keepdims=True))
    alpha = jnp.exp2(m_prev - m_new)
    p = jnp.exp2(u - _lane_tile(m_new, keys))
    l_new = alpha * l_prev + jnp.sum(p, axis=-1, keepdims=True)
    acc_new = alpha * acc_prev + jnp.dot(p.astype(BF16), v, preferred_element_type=F32)
    return m_new, l_new, acc_new


def _fox_prompt_kernel(q_ref, k_ref, v_ref, ft_ref, g_ref, o_ref, m_ref, l_ref, acc_ref, *, tk):
    tq = q_ref.shape[0]
    h = pl.program_id(0)
    qi = pl.program_id(1)
    m_ref[...] = jnp.full_like(m_ref, NEG_INF)
    l_ref[...] = jnp.zeros_like(l_ref)
    acc_ref[...] = jnp.zeros_like(acc_ref)

    def scores(r0, k0, causal):
        u = lax.dot_general(q_ref[r0:, :], k_ref[pl.ds(k0, tk), :], NT_DIMS, preferred_element_type=F32)
        u = u - ft_ref[pl.ds(h, 1), pl.ds(k0, tk)]
        if causal:
            r = lax.broadcasted_iota(jnp.int32, u.shape, 0)
            c = lax.broadcasted_iota(jnp.int32, u.shape, 1)
            u = jnp.where(c <= r, u, NEG_INF)
        return u

    def update(r0, k0, u):
        v = v_ref[pl.ds(k0, tk), :]
        m_ref[r0:], l_ref[r0:], acc_ref[r0:] = _fox_update(u, v, m_ref[r0:], l_ref[r0:], acc_ref[r0:])

    def body(j, carry):
        ka = pl.multiple_of(j * 2 * tk, tk)
        kb = pl.multiple_of(j * 2 * tk + tk, tk)
        ua = scores(0, ka, False)
        ub = scores(0, kb, False)
        update(0, ka, ua)
        update(0, kb, ub)
        return carry

    lax.fori_loop(0, qi * (tq // (2 * tk)), body, 0)
    for d in range(tq // tk):
        k0 = pl.multiple_of(qi * tq + d * tk, tk)
        update(d * tk, k0, scores(d * tk, k0, True))
    o = acc_ref[...] / l_ref[...]
    o_ref[...] = _head_rms(o, g_ref[...]).astype(o_ref.dtype)


def fox_prompt(q, k, v, ft, g_head, tq, tk):
    t = q.shape[0]
    return pl.pallas_call(
        functools.partial(_fox_prompt_kernel, tk=tk),
        grid=(H_A, t // tq),
        in_specs=[pl.BlockSpec((tq, HEAD_DIM), lambda h, i: (i, h)),
                  pl.BlockSpec((t, HEAD_DIM), lambda h, i: (0, h)),
                  pl.BlockSpec((t, HEAD_DIM), lambda h, i: (0, h)),
                  pl.BlockSpec(ft.shape, lambda h, i: (0, 0)),
                  pl.BlockSpec((1, HEAD_DIM), lambda h, i: (0, h))],
        out_specs=pl.BlockSpec((tq, HEAD_DIM), lambda h, i: (i, h)),
        out_shape=jax.ShapeDtypeStruct((t, D_A), BF16),
        scratch_shapes=[pltpu.VMEM((tq, LANES), F32), pltpu.VMEM((tq, LANES), F32),
                        pltpu.VMEM((tq, HEAD_DIM), F32)],
        compiler_params=_params("parallel", "arbitrary"),
        name="fox_prompt",
    )(q, k, v, ft, g_head)


def _neg_abs(x):
    bits = lax.bitcast_convert_type(x, jnp.uint32) | jnp.uint32(0x80000000)
    return lax.bitcast_convert_type(bits, F32)


def _stick_scores(q, k, suffix_mat, visible):
    nz = lax.dot_general(q, k, NT_DIMS, preferred_element_type=F32)
    log_keep = jnp.minimum(nz, 0.0) - jnp.log2(1.0 + jnp.exp2(_neg_abs(nz)))
    if visible is not None:
        log_keep = jnp.where(visible, log_keep, 0.0)
    hi = log_keep.astype(BF16)
    lo = (log_keep - hi.astype(F32)).astype(BF16)
    suffix = jnp.dot(jnp.concatenate([hi, lo], axis=1), suffix_mat, preferred_element_type=F32)
    return nz, log_keep, suffix


def _stick_weights(nz, log_keep, suffix, carry, visible):
    later = suffix + _lane_tile(carry, nz.shape[1])
    w = jnp.exp2(log_keep - nz + later)
    if visible is not None:
        w = jnp.where(visible, w, 0.0)
    return w.astype(BF16), carry + jnp.sum(log_keep, axis=-1, keepdims=True)


def _stick_block(q, k, v, suffix_mat, carry, visible):
    nz, log_keep, suffix = _stick_scores(q, k, suffix_mat, visible)
    w, carry = _stick_weights(nz, log_keep, suffix, carry, visible)
    return jnp.dot(w, v, preferred_element_type=F32), carry


def _suffix_mat(n):
    r = lax.broadcasted_iota(jnp.int32, (2 * n, n), 0)
    c = lax.broadcasted_iota(jnp.int32, (2 * n, n), 1)
    return jnp.where(jnp.where(r >= n, r - n, r) > c, 1.0, 0.0).astype(BF16)


def _stick_prompt_kernel(q_ref, k_ref, v_ref, g_ref, o_ref, c_ref, acc_ref):
    tq = q_ref.shape[0]
    tb = CUM_BLOCK
    qi = pl.program_id(1)
    q = q_ref[...]
    suffix_mat = _suffix_mat(tb)
    c_ref[...] = jnp.zeros_like(c_ref)
    acc_ref[...] = jnp.zeros_like(acc_ref)

    def step(k0, diag_offset):
        visible = None
        if diag_offset is not None:
            r = lax.broadcasted_iota(jnp.int32, (tq, tb), 0)
            c = lax.broadcasted_iota(jnp.int32, (tq, tb), 1)
            visible = c + diag_offset < r
        pv, carry = _stick_block(q, k_ref[pl.ds(k0, tb), :], v_ref[pl.ds(k0, tb), :],
                                 suffix_mat, c_ref[...], visible)
        acc_ref[...] += pv
        c_ref[...] = carry

    for d in reversed(range(tq // tb)):
        step(pl.multiple_of(qi * tq + d * tb, tb), d * tb)
    n_pairs = qi * (tq // (2 * tb))

    def body(j, carry):
        k_lo = pl.multiple_of((n_pairs - 1 - j) * 2 * tb, 2 * tb)
        k_hi = pl.multiple_of(k_lo + tb, tb)
        s_hi = _stick_scores(q, k_ref[pl.ds(k_hi, tb), :], suffix_mat, None)
        s_lo = _stick_scores(q, k_ref[pl.ds(k_lo, tb), :], suffix_mat, None)
        w_hi, c_mid = _stick_weights(*s_hi, c_ref[...], None)
        w_lo, c_ref[...] = _stick_weights(*s_lo, c_mid, None)
        acc_ref[...] += jnp.dot(jnp.concatenate([w_lo, w_hi], axis=1), v_ref[pl.ds(k_lo, 2 * tb), :],
                                preferred_element_type=F32)
        return carry

    lax.fori_loop(0, n_pairs, body, 0)
    o_ref[...] = _head_rms(acc_ref[...], g_ref[...]).astype(o_ref.dtype)


def stick_prompt(q, k, v, g_head, tq):
    t = q.shape[0]
    return pl.pallas_call(
        _stick_prompt_kernel,
        grid=(H_B, t // tq),
        in_specs=[pl.BlockSpec((tq, HEAD_DIM), lambda h, i: (i, h)),
                  pl.BlockSpec((t, HEAD_DIM), lambda h, i: (0, h)),
                  pl.BlockSpec((t, HEAD_DIM), lambda h, i: (0, h)),
                  pl.BlockSpec((1, HEAD_DIM), lambda h, i: (0, H_A + h))],
        out_specs=pl.BlockSpec((tq, HEAD_DIM), lambda h, i: (i, h)),
        out_shape=jax.ShapeDtypeStruct((t, D_B), BF16),
        scratch_shapes=[pltpu.VMEM((tq, LANES), F32), pltpu.VMEM((tq, HEAD_DIM), F32)],
        compiler_params=_params("parallel", "arbitrary"),
        name="stick_prompt",
    )(q, k, v, g_head)


def _band_table_kernel(rb_ref, o_ref):
    rows, cols = o_ref.shape[1], o_ref.shape[2]
    m = lax.broadcasted_iota(jnp.int32, (REL_PAD, cols), 1)
    e = lax.broadcasted_iota(jnp.int32, (REL_PAD, cols), 0)
    d = ((BAND_ROWS - m + (CHUNK - 1)) & (cols - 1)) - (CHUNK - 1)
    idx = jnp.clip(d, -REL_CLIP, REL_CLIP) + REL_CLIP
    row0 = jnp.sum(jnp.where(e == idx, rb_ref[0], 0.0), axis=0, keepdims=True)
    x = jnp.broadcast_to(row0, (rows, cols))
    i = lax.broadcasted_iota(jnp.int32, (rows, cols), 0)
    j = lax.broadcasted_iota(jnp.int32, (rows, cols), 1)
    shift = 1
    while shift < rows:
        x = jnp.where((i & shift) != 0, pltpu.roll(x, shift, axis=1), x)
        shift *= 2
    qc = i // CHUNK
    kc = j // CHUNK - N_PREV_CHUNKS
    allowed = (kc <= qc) & (kc >= qc - N_PREV_CHUNKS)
    o_ref[0] = jnp.where(allowed, x, NEG_INF)


def band_table(rel_bias):
    rb = jnp.pad(rel_bias.T, ((0, 0), (0, REL_PAD - rel_bias.shape[0])))[:, :, None]
    return pl.pallas_call(
        _band_table_kernel,
        grid=(H_C,),
        in_specs=[pl.BlockSpec((1, REL_PAD, 1), lambda h: (h, 0, 0))],
        out_specs=pl.BlockSpec((1, BAND_ROWS, 2 * BAND_ROWS), lambda h: (h, 0, 0)),
        out_shape=jax.ShapeDtypeStruct((H_C, BAND_ROWS, 2 * BAND_ROWS), F32),
        compiler_params=_params("parallel"),
        name="band_table",
    )(rb)


def _band_prompt_kernel(q_ref, kp_ref, kc_ref, vp_ref, vc_ref, tab_ref, g_ref, o_ref):
    tq = q_ref.shape[0]
    qi = pl.program_id(1)
    q = q_ref[...]
    s_prev = lax.dot_general(q, kp_ref[...], NT_DIMS, preferred_element_type=F32)
    s_cur = lax.dot_general(q, kc_ref[...], NT_DIMS, preferred_element_type=F32)
    s_prev = jnp.where(qi > 0, s_prev + tab_ref[0, :, :tq], NEG_INF)
    s_cur = s_cur + tab_ref[0, :, tq:]
    m = jnp.maximum(jnp.max(s_prev, axis=-1, keepdims=True), jnp.max(s_cur, axis=-1, keepdims=True))
    p_prev = jnp.exp(s_prev - m)
    p_cur = jnp.exp(s_cur - m)
    l = jnp.sum(p_prev, axis=-1, keepdims=True) + jnp.sum(p_cur, axis=-1, keepdims=True)
    o = (jnp.dot(p_prev.astype(BF16), vp_ref[...], preferred_element_type=F32) +
         jnp.dot(p_cur.astype(BF16), vc_ref[...], preferred_element_type=F32)) / l
    o_ref[...] = _head_rms(o, g_ref[...]).astype(o_ref.dtype)


def band_prompt(q, k, v, table, g_head):
    t = q.shape[0]
    tq = BAND_ROWS
    cur = lambda h, i: (i, h)
    prev = lambda h, i: (jnp.maximum(i - 1, 0), h)
    blk = lambda f: pl.BlockSpec((tq, HEAD_DIM), f)
    return pl.pallas_call(
        _band_prompt_kernel,
        grid=(H_C, t // tq),
        in_specs=[blk(cur), blk(prev), blk(cur), blk(prev), blk(cur),
                  pl.BlockSpec((1, tq, 2 * tq), lambda h, i: (h, 0, 0)),
                  pl.BlockSpec((1, HEAD_DIM), lambda h, i: (0, H_A + H_B + h))],
        out_specs=blk(cur),
        out_shape=jax.ShapeDtypeStruct((t, D_C), BF16),
        compiler_params=_params("parallel", "arbitrary"),
        name="band_prompt",
    )(q, k, k, v, v, table, g_head)


def _cache_head(cache_ref, r0, rows, n_heads, h, kv):
    stride = 2 * n_heads
    return cache_ref[pl.ds(r0 * stride + 2 * h + kv, rows, stride=stride), :].astype(BF16)


def _own_rows(tq, n_keys, first, strict):
    r = lax.broadcasted_iota(jnp.int32, (tq, n_keys), 0)
    c = lax.broadcasted_iota(jnp.int32, (tq, n_keys), 1) - first
    return (c >= 0) & ((c < r) if strict else (c <= r))


def _fox_sample_kernel(q_ref, cache_ref, kn_ref, vn_ref, lfc_ref, lfn_ref, g_ref, o_ref,
                       m_ref, l_ref, acc_ref, fcar_ref, f_ref):
    i = pl.program_id(0)
    j = pl.program_id(1)
    tq = q_ref.shape[0]
    tk = f_ref.shape[1]
    tb = CUM_BLOCK

    @pl.when(j == 0)
    def _():
        m_ref[...] = jnp.full_like(m_ref, NEG_INF)
        l_ref[...] = jnp.zeros_like(l_ref)
        acc_ref[...] = jnp.zeros_like(acc_ref)
        fcar_ref[...] = jnp.zeros_like(fcar_ref)

    x = lfc_ref[...].reshape(H_A * SUBLANES, tk)
    upper = _tri(tb, strict=False)
    carry = fcar_ref[...]
    for sb in range(tk // tb):
        cum = _lane_tile(carry, tb) + _dot_exact(x[:, sb * tb:(sb + 1) * tb], upper)
        f_ref[:, sb * tb:(sb + 1) * tb] = cum * LOG2E
        carry = jnp.broadcast_to(cum[:, tb - 1:tb], carry.shape)
    fcar_ref[...] = carry

    def update(h, u, v):
        m_ref[h], l_ref[h], acc_ref[h] = _fox_update(u, v, m_ref[h], l_ref[h], acc_ref[h])

    for h in range(H_A):
        q = q_ref[:, h * HEAD_DIM:(h + 1) * HEAD_DIM]
        k = _cache_head(cache_ref, 0, tk, H_A, h, 0)
        v = _cache_head(cache_ref, 0, tk, H_A, h, 1)
        u = lax.dot_general(q, k, NT_DIMS, preferred_element_type=F32)
        update(h, u - f_ref[pl.ds(h * SUBLANES + i, 1), :], v)

    @pl.when(j == pl.num_programs(1) - 1)
    def _():
        n_keys = kn_ref.shape[0]
        visible = _own_rows(tq, n_keys, i * tq, strict=False)
        r = lax.broadcasted_iota(jnp.int32, (n_keys, n_keys), 0)
        c = lax.broadcasted_iota(jnp.int32, (n_keys, n_keys), 1)
        seg = jnp.where((r <= c) & (r >= (c // tq) * tq), 1.0, 0.0).astype(BF16)
        f_new = _dot_exact(lfn_ref[...], seg)
        outs = []
        for h in range(H_A):
            lo, hi = h * HEAD_DIM, (h + 1) * HEAD_DIM
            f_row = (f_new[h:h + 1, :] + fcar_ref[pl.ds(h * SUBLANES + i, 1), :]) * LOG2E
            u = lax.dot_general(q_ref[:, lo:hi], kn_ref[:, lo:hi], NT_DIMS, preferred_element_type=F32)
            update(h, jnp.where(visible, u - f_row, NEG_INF), vn_ref[:, lo:hi])
            outs.append(_head_rms(acc_ref[h] / l_ref[h], g_ref[:, lo:hi]))
        o_ref[...] = jnp.concatenate(outs, axis=-1).astype(o_ref.dtype)


def fox_sample(q, cache, k_new, v_new, lf_cache, lf_new, g_head, layer, n_new, tk):
    _, b, rows, _ = cache.shape
    past = rows // (2 * H_A)
    ms = q.shape[0]
    return pl.pallas_call(
        _fox_sample_kernel,
        grid=(b, past // tk),
        in_specs=[pl.BlockSpec((n_new, D_A), lambda i, j: (i, 0)),
                  pl.BlockSpec((None, None, tk * 2 * H_A, HEAD_DIM), lambda i, j: (layer, i, j, 0)),
                  pl.BlockSpec((ms, D_A), lambda i, j: (0, 0)),
                  pl.BlockSpec((ms, D_A), lambda i, j: (0, 0)),
                  pl.BlockSpec((None, H_A, b, tk), lambda i, j: (layer, 0, 0, j)),
                  pl.BlockSpec((None, SUBLANES, ms), lambda i, j: (layer, 0, 0)),
                  pl.BlockSpec((1, D_MIX), lambda i, j: (0, 0))],
        out_specs=pl.BlockSpec((n_new, D_A), lambda i, j: (i, 0)),
        out_shape=jax.ShapeDtypeStruct((ms, D_A), BF16),
        scratch_shapes=[pltpu.VMEM((H_A, n_new, LANES), F32), pltpu.VMEM((H_A, n_new, LANES), F32),
                        pltpu.VMEM((H_A, n_new, HEAD_DIM), F32),
                        pltpu.VMEM((H_A * b, LANES), F32), pltpu.VMEM((H_A * b, tk), F32)],
        compiler_params=_params("parallel", "arbitrary"),
        name="fox_sample",
    )(q, cache, k_new, v_new, lf_cache, lf_new, g_head)


def _stick_sample_kernel(q_ref, cache_ref, kn_ref, vn_ref, g_ref, o_ref, c_ref, acc_ref, *, tk):
    i = pl.program_id(0)
    j = pl.program_id(1)
    tq = q_ref.shape[0]
    tb = CUM_BLOCK
    suffix_mat = _suffix_mat(tb)

    @pl.when(j == 0)
    def _():
        n_keys = kn_ref.shape[0]
        visible = _own_rows(tq, n_keys, i * tq, strict=True)
        new_mat = _suffix_mat(n_keys)
        for h in range(H_B):
            lo, hi = h * HEAD_DIM, (h + 1) * HEAD_DIM
            pv, carry = _stick_block(q_ref[:, lo:hi], kn_ref[:, lo:hi], vn_ref[:, lo:hi],
                                     new_mat, jnp.zeros((tq, LANES), F32), visible)
            acc_ref[h] = pv
            c_ref[h] = carry

    for h in range(H_B):
        q = q_ref[:, h * HEAD_DIM:(h + 1) * HEAD_DIM]
        for sb in reversed(range(tk // tb)):
            k = _cache_head(cache_ref, sb * tb, tb, H_B, h, 0)
            v = _cache_head(cache_ref, sb * tb, tb, H_B, h, 1)
            pv, carry = _stick_block(q, k, v, suffix_mat, c_ref[h], None)
            acc_ref[h] += pv
            c_ref[h] = carry

    @pl.when(j == pl.num_programs(1) - 1)
    def _():
        outs = [_head_rms(acc_ref[h], g_ref[:, D_A + h * HEAD_DIM:D_A + (h + 1) * HEAD_DIM])
                for h in range(H_B)]
        o_ref[...] = jnp.concatenate(outs, axis=-1).astype(o_ref.dtype)


def stick_sample(q, cache, k_new, v_new, g_head, layer, n_new, tk):
    _, b, rows, _ = cache.shape
    nk = rows // (2 * H_B) // tk
    ms = q.shape[0]
    return pl.pallas_call(
        functools.partial(_stick_sample_kernel, tk=tk),
        grid=(b, nk),
        in_specs=[pl.BlockSpec((n_new, D_B), lambda i, j: (i, 0)),
                  pl.BlockSpec((None, None, tk * 2 * H_B, HEAD_DIM), lambda i, j: (layer, i, nk - 1 - j, 0)),
                  pl.BlockSpec((ms, D_B), lambda i, j: (0, 0)),
                  pl.BlockSpec((ms, D_B), lambda i, j: (0, 0)),
                  pl.BlockSpec((1, D_MIX), lambda i, j: (0, 0))],
        out_specs=pl.BlockSpec((n_new, D_B), lambda i, j: (i, 0)),
        out_shape=jax.ShapeDtypeStruct((ms, D_B), BF16),
        scratch_shapes=[pltpu.VMEM((H_B, n_new, LANES), F32), pltpu.VMEM((H_B, n_new, HEAD_DIM), F32)],
        compiler_params=_params("parallel", "arbitrary"),
        name="stick_sample",
    )(q, cache, k_new, v_new, g_head)


def _band_sample_kernel(q_ref, cache_ref, kn_ref, vn_ref, new_ref, tab_ref, g_ref, *refs, keep):
    o_ref, kv_ref = refs[-2:]
    i = pl.program_id(0)
    tq = q_ref.shape[0]
    n_keys = kn_ref.shape[0]
    stride = 2 * H_C
    c = lax.broadcasted_iota(jnp.int32, (tq, n_keys), 1) - i * tq
    own = (c >= 0) & (c < tq)
    outs = []
    for h in range(H_C):
        lo, hi = h * HEAD_DIM, (h + 1) * HEAD_DIM
        q = q_ref[:, lo:hi]
        k = _cache_head(cache_ref, 0, keep, H_C, h, 0)
        v = _cache_head(cache_ref, 0, keep, H_C, h, 1)
        s_c = lax.dot_general(q, k, NT_DIMS, preferred_element_type=F32) + tab_ref[h, :, :keep]
        s_n = lax.dot_general(q, kn_ref[:, lo:hi], NT_DIMS, preferred_element_type=F32)
        bias_n = pltpu.roll(tab_ref[h, :, keep:], i * tq, axis=1)
        s_n = jnp.where(own, s_n + bias_n, NEG_INF)
        m = jnp.maximum(jnp.max(s_c, axis=-1, keepdims=True), jnp.max(s_n, axis=-1, keepdims=True))
        p_c = jnp.exp(s_c - m)
        p_n = jnp.exp(s_n - m)
        l = jnp.sum(p_c, axis=-1, keepdims=True) + jnp.sum(p_n, axis=-1, keepdims=True)
        o = (jnp.dot(p_c.astype(BF16), v, preferred_element_type=F32) +
             jnp.dot(p_n.astype(BF16), vn_ref[:, lo:hi], preferred_element_type=F32)) / l
        outs.append(_head_rms(o, g_ref[:, D_A + D_B + lo:D_A + D_B + hi]))
    o_ref[...] = jnp.concatenate(outs, axis=-1).astype(o_ref.dtype)
    kv_ref[:(keep - tq) * stride, :] = cache_ref[tq * stride:, :]
    kv_ref[(keep - tq) * stride:, :] = new_ref[...]


def band_sample(q, cache, k_new, v_new, kv_new, table, g_head, stacked, layer, n_new):
    depth, b, rows, _ = cache.shape
    keep = rows // (2 * H_C)
    ms = q.shape[0]
    in_specs = [pl.BlockSpec((n_new, D_C), lambda i: (i, 0)),
                pl.BlockSpec((None, None, rows, HEAD_DIM), lambda i: (layer, i, 0, 0)),
                pl.BlockSpec((ms, D_C), lambda i: (0, 0)),
                pl.BlockSpec((ms, D_C), lambda i: (0, 0)),
                pl.BlockSpec((None, n_new * 2 * H_C, HEAD_DIM), lambda i: (layer, i, 0)),
                pl.BlockSpec((H_C, n_new, keep + ms), lambda i: (0, 0, 0)),
                pl.BlockSpec((1, D_MIX), lambda i: (0, 0))]
    args = [q, cache, k_new, v_new, kv_new, table, g_head]
    aliases = {}
    if stacked is not None:
        in_specs.append(pl.BlockSpec(memory_space=pl.ANY))
        aliases = {len(args): 1}
        args.append(stacked)
    return pl.pallas_call(
        functools.partial(_band_sample_kernel, keep=keep),
        grid=(b,),
        in_specs=in_specs,
        out_specs=[pl.BlockSpec((n_new, D_C), lambda i: (i, 0)),
                   pl.BlockSpec((None, None, rows, HEAD_DIM), lambda i: (layer, i, 0, 0))],
        out_shape=[jax.ShapeDtypeStruct((ms, D_C), BF16),
                   jax.ShapeDtypeStruct((depth, b, rows, HEAD_DIM), F32)],
        input_output_aliases=aliases,
        compiler_params=_params("arbitrary"),
        name="band_sample",
    )(*args)


def _outproj_kernel(x_ref, oa_ref, ob_ref, oc_ref, w_ref, g_ref, y_ref, hn_ref):
    y = x_ref[...]
    y = y + jnp.dot(oa_ref[...], w_ref[:D_A, :], preferred_element_type=F32)
    y = y + jnp.dot(ob_ref[...], w_ref[D_A:D_A + D_B, :], preferred_element_type=F32)
    y = y + jnp.dot(oc_ref[...], w_ref[D_A + D_B:, :], preferred_element_type=F32)
    y_ref[...] = y
    hn_ref[...] = _row_rms(y, g_ref[...], hn_ref.dtype)


def output_projection(x, oa, ob, oc, w_out, g_mlp, layer, tm):
    m, d = x.shape
    row = lambda i: (i, 0)
    return pl.pallas_call(
        _outproj_kernel,
        grid=(m // tm,),
        in_specs=[pl.BlockSpec((tm, d), row), pl.BlockSpec((tm, D_A), row),
                  pl.BlockSpec((tm, D_B), row), pl.BlockSpec((tm, D_C), row),
                  pl.BlockSpec((None, D_MIX, d), lambda i: (layer, 0, 0), pipeline_mode=pl.Buffered(1)),
                  pl.BlockSpec((1, d), lambda i: (0, 0))],
        out_specs=[pl.BlockSpec((tm, d), row), pl.BlockSpec((tm, d), row)],
        out_shape=[jax.ShapeDtypeStruct((m, d), F32), jax.ShapeDtypeStruct((m, d), BF16)],
        compiler_params=_params("parallel"),
        name="outproj",
    )(x, oa, ob, oc, w_out, g_mlp)


def _mlp_kernel(h_ref, y_ref, wu_ref, wd_ref, g_ref, o_ref, xn_ref, acc_ref):
    f = pl.program_id(1)

    @pl.when(f == 0)
    def _():
        acc_ref[...] = y_ref[...]

    u = jnp.dot(h_ref[...], wu_ref[...], preferred_element_type=F32)
    u = jnp.square(jnp.maximum(u, 0.0)).astype(BF16)
    acc_ref[...] += jnp.dot(u, wd_ref[...], preferred_element_type=F32)

    @pl.when(f == pl.num_programs(1) - 1)
    def _():
        y = acc_ref[...]
        o_ref[...] = y
        xn_ref[...] = _row_rms(y, g_ref[...], xn_ref.dtype)


def mlp_block(hn, y, w_up, w_down, g_next, layer, tm, tf):
    m, d = y.shape
    d_ff = w_up.shape[2]
    return pl.pallas_call(
        _mlp_kernel,
        grid=(m // tm, d_ff // tf),
        in_specs=[pl.BlockSpec((tm, d), lambda i, f: (i, 0)),
                  pl.BlockSpec((tm, d), lambda i, f: (i, 0)),
                  pl.BlockSpec((None, d, tf), lambda i, f: (layer, 0, f)),
                  pl.BlockSpec((None, tf, d), lambda i, f: (layer, f, 0)),
                  pl.BlockSpec((1, d), lambda i, f: (0, 0))],
        out_specs=[pl.BlockSpec((tm, d), lambda i, f: (i, 0)),
                   pl.BlockSpec((tm, d), lambda i, f: (i, 0))],
        out_shape=[jax.ShapeDtypeStruct((m, d), F32), jax.ShapeDtypeStruct((m, d), BF16)],
        scratch_shapes=[pltpu.VMEM((tm, d), F32)],
        compiler_params=_params("parallel", "arbitrary"),
        name="mlp",
    )(hn, y, w_up, w_down, g_next)


def _tiles(t_prompt):
    return dict(
        norm=min(512, t_prompt), inproj=min(256, t_prompt), fox_q=min(1024, t_prompt),
        fox_k=min(1024, t_prompt) // 2,
        stick_q=min(512, t_prompt), prefix=min(512, t_prompt), outproj=min(512, t_prompt),
        mlp_m=min(512, t_prompt), mlp_f=1024, sample_k=1024)


def _flat_cache(c):
    depth, b, rows, two, h, dh = c.shape
    return jnp.transpose(c, (0, 1, 2, 4, 3, 5)).reshape(depth, b, rows * two * h, dh)


def _unflat_cache(flat, b, rows, h):
    depth = flat.shape[0]
    return jnp.transpose(flat.reshape(depth, b, rows, h, 2, HEAD_DIM), (0, 1, 2, 4, 3, 5))


def kernel(x_prompt, x_sample, cache_a_kv, cache_a_logf, cache_b_kv, cache_c_kv, norm_mix, norm_mlp,
           w_in, b_forget, g_q_fox, g_k_fox, g_q_band, g_k_band, rel_bias, g_head_out, w_out, w_up,
           w_down):
    depth = w_in.shape[0]
    bp, t, d = x_prompt.shape
    bs, ts, _ = x_sample.shape
    past = cache_a_kv.shape[2]
    keep_s = cache_c_kv.shape[2]
    keep_p = min(BAND_ROWS, t)
    ms = bs * ts
    assert bp == 1 and t % BAND_ROWS == 0, "prompt group: one stream, whole band tiles"
    assert past % CHUNK == 0 and keep_s == BAND_ROWS and ts <= CHUNK and ts % SUBLANES == 0
    assert bs == SUBLANES and ms == LANES, "sample group: streams fill a sublane tile, new rows a lane tile"
    assert w_in.shape[2] == 3 * D_MIX + H_A
    tl = _tiles(t)
    tk_s = min(tl["sample_k"], past)

    w_in_b = w_in[:, :, :3 * D_MIX].astype(BF16)
    w_f = jnp.pad(jnp.swapaxes(w_in[:, :, 3 * D_MIX:], 1, 2), ((0, 0), (0, SUBLANES - H_A), (0, 0))).astype(BF16)
    b_f = jnp.pad(b_forget, ((0, 0), (0, SUBLANES - H_A)))[:, :, None]
    w_out_b = w_out.astype(BF16)
    w_up_b = w_up.astype(BF16)
    w_down_b = w_down.astype(BF16)
    cache_a = _flat_cache(cache_a_kv)
    cache_b = _flat_cache(cache_b_kv)
    cache_c = _flat_cache(cache_c_kv)
    cache_lf = jnp.transpose(cache_a_logf, (0, 3, 1, 2))

    y_p = x_prompt.reshape(t, d)
    y_s = x_sample.reshape(ms, d)
    xn_p = rmsnorm_bf16(y_p, norm_mix[0][None], tl["norm"])
    xn_s = rmsnorm_bf16(y_s, norm_mix[0][None], ms)
    stacked_p = stacked_s = rolled = None

    for l in range(depth):
        vecs = (g_q_fox[l][None], g_k_fox[l][None], g_q_band[l][None], g_k_band[l][None])
        g_head = g_head_out[l][None]
        g_next = norm_mix[(l + 1) % depth][None]
        table = band_table(rel_bias[l])

        qa, ka, va, qb, kb, vb, qc, kc, vc, *stacked_p = input_projection(
            xn_p, w_in_b, w_f, b_f, *vecs, stacked_p, l, tm=tl["inproj"], keep=keep_p)
        ft = forget_prefix(stacked_p[3], l, tl["prefix"])
        oa = fox_prompt(qa, ka, va, ft, g_head, tl["fox_q"], tl["fox_k"])
        ob = stick_prompt(qb, kb, vb, g_head, tl["stick_q"])
        oc = band_prompt(qc, kc, vc, table, g_head)
        y_p, hn = output_projection(y_p, oa, ob, oc, w_out_b, norm_mlp[l][None], l, tl["outproj"])
        y_p, xn_p = mlp_block(hn, y_p, w_up_b, w_down_b, g_next, l, tl["mlp_m"], tl["mlp_f"])

        qa, ka, va, qb, kb, vb, qc, kc, vc, *stacked_s = input_projection(
            xn_s, w_in_b, w_f, b_f, *vecs, stacked_s, l, tm=ms, keep=ms)
        oa = fox_sample(qa, cache_a, ka, va, cache_lf, stacked_s[3], g_head, l, ts, tk_s)
        ob = stick_sample(qb, cache_b, kb, vb, g_head, l, ts, tk_s)
        oc, rolled = band_sample(qc, cache_c, kc, vc, stacked_s[2], table, g_head, rolled, l, ts)
        y_s, hn = output_projection(y_s, oa, ob, oc, w_out_b, norm_mlp[l][None], l, ms)
        y_s, xn_s = mlp_block(hn, y_s, w_up_b, w_down_b, g_next, l, ms, tl["mlp_f"])

    def logf_out(lf_t, b, rows):
        return jnp.transpose(lf_t[:, :H_A, :].reshape(depth, H_A, b, rows), (0, 2, 3, 1))

    return (y_p.reshape(bp, t, d), y_s.reshape(bs, ts, d),
            _unflat_cache(stacked_p[0], bp, t, H_A), logf_out(stacked_p[3], bp, t),
            _unflat_cache(stacked_p[1], bp, t, H_B), _unflat_cache(stacked_p[2], bp, keep_p, H_C),
            _unflat_cache(stacked_s[0], bs, ts, H_A), logf_out(stacked_s[3], bs, ts),
            _unflat_cache(stacked_s[1], bs, ts, H_B),
            _unflat_cache(rolled.reshape(depth, bs * keep_s * 2 * H_C, HEAD_DIM), bs, keep_s, H_C))
```
